```python
import jax, jax.numpy as jnp
from jax import lax
import numpy as np

D_MODEL = 2048
BATCH = 16
SEQ = 2048
DEPTH = 1
DEC_BATCH = 4
DEC_SEQ = 4096
PAST_LEN = 128

HEAD_DIM = 128
HEADS_PER_GROUP = 4
DILATED_GROUPS = ((128, 1), (512, 4), (2048, 16))
N_GROUPS = 3
N_ATTN_HEADS = N_GROUPS * HEADS_PER_GROUP
ATTN_WIDTH = N_ATTN_HEADS * HEAD_DIM
ATTN_OUT_WIDTH = HEADS_PER_GROUP * HEAD_DIM
CONV_WIDTH = D_MODEL
CONV_TAPS = 3
ROPE_THETA = 10000.0
NORM_EPS = 1e-6
PEER_HEADS = 8
PEER_NKEYS = 128
PEER_EXPERTS = PEER_NKEYS * PEER_NKEYS
PEER_QDIM = 256
PEER_HALF = PEER_QDIM // 2
PEER_TOPK = 16
PEER_CHUNK = 128
IN_COLS = 3 * ATTN_WIDTH + 3 * CONV_WIDTH + 2 * D_MODEL

kernel_name = "hybrid_dilated_attn_shortconv_peer_encoder"

F32 = jnp.float32


def _rms_norm(x, g):
    xf = x.astype(F32)
    y = xf * lax.rsqrt(jnp.mean(xf * xf, axis=-1, keepdims=True) + NORM_EPS) * g.astype(F32)
    return y.astype(x.dtype)


def _rotary(x, positions):
    d = x.shape[-1]
    half = d // 2
    inv_freq = ROPE_THETA ** (-jnp.arange(half, dtype=F32) * 2.0 / d)
    ang = positions.astype(F32)[:, None] * inv_freq[None, :]
    cos = jnp.cos(ang)[None, :, None, :]
    sin = jnp.sin(ang)[None, :, None, :]
    xf = x.astype(F32)
    x1, x2 = xf[..., :half], xf[..., half:]
    return jnp.concatenate([x1 * cos - x2 * sin, x2 * cos + x1 * sin], axis=-1).astype(x.dtype)


def _dilated_band_attention(q, k, v, window, dilation):
    b, s, h, d = q.shape
    r = dilation
    half = (window // 2) // r
    blk = half
    l = s // r
    lp = -(-l // blk) * blk
    nb = lp // blk

    def to_classes(t):
        t = t.reshape(b, l, r, h, d).transpose(0, 2, 3, 1, 4)
        return jnp.pad(t, ((0, 0), (0, 0), (0, 0), (0, lp - l), (0, 0)))

    def neighbour_blocks(t):
        tp = jnp.pad(t, ((0, 0), (0, 0), (0, 0), (blk, blk), (0, 0))).reshape(b, r, h, nb + 2, blk, d)
        return jnp.concatenate([tp[:, :, :, :-2], tp[:, :, :, 1:-1], tp[:, :, :, 2:]], axis=-2)

    qb = to_classes(q).reshape(b, r, h, nb, blk, d)
    kb = neighbour_blocks(to_classes(k))
    vb = neighbour_blocks(to_classes(v))

    q_idx = jnp.arange(nb)[:, None] * blk + jnp.arange(blk)[None, :]
    k_idx = (jnp.arange(nb)[:, None] - 1) * blk + jnp.arange(3 * blk)[None, :]
    rel = k_idx[:, None, :] - q_idx[:, :, None]
    valid = (jnp.abs(rel) <= half) & (k_idx[:, None, :] >= 0) & (k_idx[:, None, :] < l)

    scores = jnp.einsum('brhnqd,brhnkd->brhnqk', qb.astype(F32), kb.astype(F32)) * (d ** -0.5)
    scores = jnp.where(valid, scores, -jnp.inf)
    m = jnp.max(scores, axis=-1, keepdims=True)
    p = jnp.exp(scores - m)
    den = jnp.sum(p, axis=-1, keepdims=True)
    o = jnp.einsum('brhnqk,brhnkd->brhnqd', p, vb.astype(F32)) / den
    lse = (m + jnp.log(den))[..., 0]

    o = o.reshape(b, r, h, lp, d)[:, :, :, :l].transpose(0, 3, 1, 2, 4).reshape(b, s, h, d)
    lse = lse.reshape(b, r, h, lp)[:, :, :, :l].transpose(0, 3, 1, 2).reshape(b, s, h)
    return o, lse


def _mixer(u, w_in, q_g, k_g, w_ao, conv_w, w_co, w_o):
    b, s, _ = u.shape
    proj = u @ w_in
    cuts = [ATTN_WIDTH, 2 * ATTN_WIDTH, 3 * ATTN_WIDTH,
            3 * ATTN_WIDTH + CONV_WIDTH, 3 * ATTN_WIDTH + 2 * CONV_WIDTH,
            3 * ATTN_WIDTH + 3 * CONV_WIDTH, 3 * ATTN_WIDTH + 3 * CONV_WIDTH + D_MODEL]
    q, k, v, cb, cc, cx, ga, gc = jnp.split(proj, cuts, axis=-1)

    pos = jnp.arange(s)
    q = _rotary(_rms_norm(q.reshape(b, s, N_ATTN_HEADS, HEAD_DIM), q_g), pos)
    k = _rotary(_rms_norm(k.reshape(b, s, N_ATTN_HEADS, HEAD_DIM), k_g), pos)
    v = v.reshape(b, s, N_ATTN_HEADS, HEAD_DIM)
    outs, lses = [], []
    for gi, (window, dilation) in enumerate(DILATED_GROUPS):
        sl = slice(gi * HEADS_PER_GROUP, (gi + 1) * HEADS_PER_GROUP)
        o, lse = _dilated_band_attention(q[:, :, sl], k[:, :, sl], v[:, :, sl], window, dilation)
        outs.append(o)
        lses.append(lse)
    alpha = jax.nn.softmax(jnp.stack(lses, axis=0), axis=0)
    attn = jnp.sum(alpha[..., None] * jnp.stack(outs, axis=0), axis=0)
    attn = attn.reshape(b, s, ATTN_OUT_WIDTH).astype(u.dtype)

    z = cc * cx
    zp = jnp.pad(z, ((0, 0), (1, 1), (0, 0)))
    zc = conv_w[0] * zp[:, :-2] + conv_w[1] * zp[:, 1:-1] + conv_w[2] * zp[:, 2:]
    conv = cb * zc

    mix = jax.nn.sigmoid(ga) * (attn @ w_ao) + jax.nn.sigmoid(gc) * (conv @ w_co)
    return mix @ w_o


def _peer(xn, w_q, sub_keys, u_tab, v_tab):
    b, s, dm = xn.shape
    t = b * s
    xf = xn.reshape(t, dm)
    q = (xf @ w_q).reshape(t, PEER_HEADS, 2, PEER_HALF)
    sc = jnp.einsum('thpd,hpnd->thpn', q.astype(F32), sub_keys.astype(F32))
    s1, i1 = lax.top_k(sc[:, :, 0], PEER_TOPK)
    s2, i2 = lax.top_k(sc[:, :, 1], PEER_TOPK)
    cand = (s1[..., :, None] + s2[..., None, :]).reshape(t, PEER_HEADS, PEER_TOPK * PEER_TOPK)
    best, flat = lax.top_k(cand, PEER_TOPK)
    e1 = jnp.take_along_axis(i1, flat // PEER_TOPK, axis=-1)
    e2 = jnp.take_along_axis(i2, flat % PEER_TOPK, axis=-1)
    experts = e1 * PEER_NKEYS + e2
    gates = jax.nn.softmax(best, axis=-1)

    nc = t // PEER_CHUNK

    def expert_block(args):
        xc, ec, gcb = args
        ug = u_tab[ec]
        vg = v_tab[ec]
        hid = jnp.einsum('cd,chkd->chk', xc, ug)
        a = (jax.nn.gelu(hid.astype(F32), approximate=False) * gcb).astype(xc.dtype)
        return jnp.einsum('chk,chkd->cd', a, vg)

    out = lax.map(expert_block, (xf.reshape(nc, PEER_CHUNK, dm),
                                 experts.reshape(nc, PEER_CHUNK, PEER_HEADS, PEER_TOPK),
                                 gates.reshape(nc, PEER_CHUNK, PEER_HEADS, PEER_TOPK)))
    return out.reshape(b, s, dm)


def _trunk(x, norm1_g, w_in, q_norm_g, k_norm_g, w_attn_out, conv_w, w_conv_out, w_o,
           norm2_g, peer_w_q, peer_sub_keys, peer_u, peer_v):
    for l in range(DEPTH):
        h = x + _mixer(_rms_norm(x, norm1_g[l]), w_in[l], q_norm_g[l], k_norm_g[l],
                       w_attn_out[l], conv_w[l], w_conv_out[l], w_o[l])
        x = h + _peer(_rms_norm(h, norm2_g[l]), peer_w_q[l], peer_sub_keys[l], peer_u[l], peer_v[l])
    return x


def setup_inputs(seed: int = 0) -> dict:
    key = jax.random.key(seed)
    ks = jax.random.split(key, 16)
    n = jax.random.normal
    return {
        "x_prompt": n(ks[0], (BATCH, SEQ, D_MODEL), F32),
        "x_sample": n(ks[1], (DEC_BATCH, DEC_SEQ, D_MODEL), F32),
        "norm1_g": 1.0 + 0.02 * n(ks[2], (DEPTH, D_MODEL), F32),
        "w_in": n(ks[3], (DEPTH, D_MODEL, IN_COLS), F32) * D_MODEL ** -0.5,
        "q_norm_g": 1.0 + 0.02 * n(ks[4], (DEPTH, HEAD_DIM), F32),
        "k_norm_g": 1.0 + 0.02 * n(ks[5], (DEPTH, HEAD_DIM), F32),
        "w_attn_out": n(ks[6], (DEPTH, ATTN_OUT_WIDTH, D_MODEL), F32) * ATTN_OUT_WIDTH ** -0.5,
        "conv_w": n(ks[7], (DEPTH, CONV_TAPS, CONV_WIDTH), F32) * CONV_TAPS ** -0.5,
        "w_conv_out": n(ks[8], (DEPTH, CONV_WIDTH, D_MODEL), F32) * CONV_WIDTH ** -0.5,
        "w_o": n(ks[9], (DEPTH, D_MODEL, D_MODEL), F32) * D_MODEL ** -0.5,
        "norm2_g": 1.0 + 0.02 * n(ks[10], (DEPTH, D_MODEL), F32),
        "peer_w_q": n(ks[11], (DEPTH, D_MODEL, PEER_HEADS * PEER_QDIM), F32) * D_MODEL ** -0.5,
        "peer_sub_keys": n(ks[12], (DEPTH, PEER_HEADS, 2, PEER_NKEYS, PEER_HALF), F32) * PEER_HALF ** -0.5,
        "peer_u": n(ks[13], (DEPTH, PEER_EXPERTS, D_MODEL), F32) * D_MODEL ** -0.5,
        "peer_v": n(ks[14], (DEPTH, PEER_EXPERTS, D_MODEL), F32) * (PEER_HEADS * PEER_TOPK) ** -0.5,
    }


def reference(x_prompt, x_sample, norm1_g, w_in, q_norm_g, k_norm_g, w_attn_out, conv_w,
              w_conv_out, w_o, norm2_g, peer_w_q, peer_sub_keys, peer_u, peer_v):
    y_prompt = _trunk(x_prompt, norm1_g, w_in, q_norm_g, k_norm_g, w_attn_out, conv_w,
                      w_conv_out, w_o, norm2_g, peer_w_q, peer_sub_keys, peer_u, peer_v)
    y_sample = _trunk(x_sample, norm1_g, w_in, q_norm_g, k_norm_g, w_attn_out, conv_w,
                      w_conv_out, w_o, norm2_g, peer_w_q, peer_sub_keys, peer_u, peer_v)
    return (y_prompt, y_sample)
```

```python
import functools

import jax
import jax.numpy as jnp
from jax import lax
from jax.experimental import pallas as pl
from jax.experimental.pallas import tpu as pltpu

F32 = jnp.float32
BF16 = jnp.bfloat16

HEAD_DIM = 128
HEADS_PER_GROUP = 4
GROUP_WIDTH = HEADS_PER_GROUP * HEAD_DIM
DILATIONS = (1, 4, 16)
N_GROUPS = len(DILATIONS)
ATTN_WIDTH = N_GROUPS * GROUP_WIDTH
BAND_HALF = 64
ROPE_THETA = 10000.0
NORM_EPS = 1e-6
PEER_HEADS = 8
PEER_NKEYS = 128
PEER_TOPK = 16
PEER_HALF = 128

COL_TILE = 512
Q_BLOCK = 128
V7X_VMEM_LIMIT = 56 * 1024 * 1024


def _params(sem, vmem=None):
    return pltpu.CompilerParams(dimension_semantics=sem, vmem_limit_bytes=vmem)


def _inproj_kernel(x_ref, g1_ref, w_ref, cos_ref, sin_ref, qg_ref, kg_ref, o_ref, xn_ref):
    j = pl.program_id(1)

    @pl.when(j == 0)
    def _():
        x = x_ref[...]
        ms = jnp.mean(x * x, axis=-1, keepdims=True)
        xn_ref[...] = (x * lax.rsqrt(ms + NORM_EPS) * g1_ref[...]).astype(BF16)

    acc = jnp.dot(xn_ref[...], w_ref[...], preferred_element_type=F32)
    n_qk_tiles = 2 * ATTN_WIDTH // COL_TILE

    @pl.when(j < n_qk_tiles)
    def _():
        g = jnp.where(j < n_qk_tiles // 2, qg_ref[...], kg_ref[...])
        for hd in range(COL_TILE // HEAD_DIM):
            cs = slice(hd * HEAD_DIM, (hd + 1) * HEAD_DIM)
            a = acc[:, cs]
            ms = jnp.mean(a * a, axis=-1, keepdims=True)
            an = a * lax.rsqrt(ms + NORM_EPS) * g
            rot = pltpu.roll(an, HEAD_DIM // 2, axis=1)
            o_ref[:, cs] = (an * cos_ref[...] + rot * sin_ref[...]).astype(BF16)

    @pl.when(j >= n_qk_tiles)
    def _():
        o_ref[...] = acc.astype(BF16)


def _inproj(x2d, g1, w_in, cos, sin, qg, kg, seq):
    t, d = x2d.shape
    nc = w_in.shape[1]
    tm = min(1024, seq)
    rb = seq // tm
    return pl.pallas_call(
        _inproj_kernel,
        out_shape=jax.ShapeDtypeStruct((t, nc), BF16),
        grid=(t // tm, nc // COL_TILE),
        in_specs=[
            pl.BlockSpec((tm, d), lambda i, j: (i, 0)),
            pl.BlockSpec((1, d), lambda i, j: (0, 0)),
            pl.BlockSpec((d, COL_TILE), lambda i, j: (0, j)),
            pl.BlockSpec((tm, HEAD_DIM), lambda i, j: (i % rb, 0)),
            pl.BlockSpec((tm, HEAD_DIM), lambda i, j: (i % rb, 0)),
            pl.BlockSpec((1, HEAD_DIM), lambda i, j: (0, 0)),
            pl.BlockSpec((1, HEAD_DIM), lambda i, j: (0, 0)),
        ],
        out_specs=pl.BlockSpec((tm, COL_TILE), lambda i, j: (i, j)),
        scratch_shapes=[pltpu.VMEM((tm, d), BF16)],
        compiler_params=_params(("arbitrary", "arbitrary"), V7X_VMEM_LIMIT),
        name="inproj",
    )(x2d, g1, w_in, cos, sin, qg, kg)


def _attn_kernel(q_ref, k_ref, v_ref, o_ref, lse_ref, *, cls_len, q_rows, kw):
    q_off = pl.program_id(2) * q_rows
    scale = HEAD_DIM ** -0.5

    def body(sb, carry):
        r0 = pl.multiple_of(sb * Q_BLOCK, Q_BLOCK)
        g0 = q_off + r0
        ks = pl.multiple_of(jnp.clip(g0 - BAND_HALF, 0, cls_len - kw), BAND_HALF)
        qpos = g0 + lax.broadcasted_iota(jnp.int32, (Q_BLOCK, kw), 0)
        kpos = ks + lax.broadcasted_iota(jnp.int32, (Q_BLOCK, kw), 1)
        valid = jnp.abs(kpos - qpos) <= BAND_HALF
        for hd in range(HEADS_PER_GROUP):
            cs = slice(hd * HEAD_DIM, (hd + 1) * HEAD_DIM)
            q = q_ref[0, pl.ds(r0, Q_BLOCK), cs]
            k = k_ref[0, pl.ds(ks, kw), cs]
            v = v_ref[0, pl.ds(ks, kw), cs]
            s = lax.dot_general(q, k, (((1,), (1,)), ((), ())), preferred_element_type=F32) * scale
            s = jnp.where(valid, s, -jnp.inf)
            m = jnp.max(s, axis=-1, keepdims=True)
            p = jnp.exp(s - m)
            den = jnp.sum(p, axis=-1, keepdims=True)
            o = jnp.dot(p.astype(BF16), v, preferred_element_type=F32) / den
            o_ref[0, pl.ds(r0, Q_BLOCK), cs] = o.astype(BF16)
            lse_ref[0, pl.ds(r0, Q_BLOCK), cs] = jnp.broadcast_to(m + jnp.log(den), (Q_BLOCK, HEAD_DIM))
        return carry

    lax.fori_loop(0, q_rows // Q_BLOCK, body, 0)


def _attention_group(proj, batch, seq, group):
    nc = proj.shape[1]
    r = DILATIONS[group]
    cls_len = seq // r
    ncb = nc // COL_TILE
    q_rows = min(cls_len, 1024)
    kw = min(cls_len, Q_BLOCK + 2 * BAND_HALF)
    view = proj.reshape(batch, cls_len, r * nc)
    n_g = ATTN_WIDTH // COL_TILE
    o, lse = pl.pallas_call(
        functools.partial(_attn_kernel, cls_len=cls_len, q_rows=q_rows, kw=kw),
        out_shape=(jax.ShapeDtypeStruct((batch, cls_len, r * GROUP_WIDTH), BF16),
                   jax.ShapeDtypeStruct((batch, cls_len, r * GROUP_WIDTH), F32)),
        grid=(batch, r, cls_len // q_rows),
        in_specs=[
            pl.BlockSpec((1, q_rows, COL_TILE), lambda b, c, qi: (b, qi, c * ncb + group)),
            pl.BlockSpec((1, cls_len, COL_TILE), lambda b, c, qi: (b, 0, c * ncb + n_g + group)),
            pl.BlockSpec((1, cls_len, COL_TILE), lambda b, c, qi: (b, 0, c * ncb + 2 * n_g + group)),
        ],
        out_specs=(pl.BlockSpec((1, q_rows, GROUP_WIDTH), lambda b, c, qi: (b, qi, c)),
                   pl.BlockSpec((1, q_rows, GROUP_WIDTH), lambda b, c, qi: (b, qi, c))),
        compiler_params=_params(("arbitrary", "arbitrary", "arbitrary"), V7X_VMEM_LIMIT),
        name=f"attn_g{group}",
    )(view, view, view)
    t = batch * seq
    return o.reshape(t, GROUP_WIDTH), lse.reshape(t, GROUP_WIDTH)


def _conv_kernel(cb_ref, cc_ref, cx_ref, ccp_ref, cxp_ref, ccn_ref, cxn_ref, w_ref, o_ref, *, blocks_per_seq):
    i = pl.program_id(0)
    tm = cc_ref.shape[0]
    z = cc_ref[...].astype(F32) * cx_ref[...].astype(F32)
    has_prev = (i % blocks_per_seq != 0).astype(F32)
    has_next = (i % blocks_per_seq != blocks_per_seq - 1).astype(F32)
    z_before = ccp_ref[7:8, :].astype(F32) * cxp_ref[7:8, :].astype(F32) * has_prev
    z_after = ccn_ref[0:1, :].astype(F32) * cxn_ref[0:1, :].astype(F32) * has_next
    row = lax.broadcasted_iota(jnp.int32, z.shape, 0)
    z_prev = jnp.where(row == 0, z_before, pltpu.roll(z, 1, axis=0))
    z_next = jnp.where(row == tm - 1, z_after, pltpu.roll(z, tm - 1, axis=0))
    zc = w_ref[0:1, :] * z_prev + w_ref[1:2, :] * z + w_ref[2:3, :] * z_next
    o_ref[...] = (cb_ref[...].astype(F32) * zc).astype(BF16)


def _conv(proj, conv_w, seq, d):
    t, nc = proj.shape
    tm = min(512, seq)
    dt = d // COL_TILE
    cb0 = ATTN_WIDTH * 3 // COL_TILE
    cc0, cx0 = cb0 + dt, cb0 + 2 * dt
    rows8 = tm // 8
    last8 = t // 8 - 1
    prev_map = lambda off: (lambda i, j: (jnp.maximum(i * rows8 - 1, 0), off + j))
    next_map = lambda off: (lambda i, j: (jnp.minimum((i + 1) * rows8, last8), off + j))
    tile = lambda off: pl.BlockSpec((tm, COL_TILE), lambda i, j: (i, off + j))
    return pl.pallas_call(
        functools.partial(_conv_kernel, blocks_per_seq=seq // tm),
        out_shape=jax.ShapeDtypeStruct((t, d), BF16),
        grid=(t // tm, dt),
        in_specs=[
            tile(cb0), tile(cc0), tile(cx0),
            pl.BlockSpec((8, COL_TILE), prev_map(cc0)), pl.BlockSpec((8, COL_TILE), prev_map(cx0)),
            pl.BlockSpec((8, COL_TILE), next_map(cc0)), pl.BlockSpec((8, COL_TILE), next_map(cx0)),
            pl.BlockSpec((3, COL_TILE), lambda i, j: (0, j)),
        ],
        out_specs=pl.BlockSpec((tm, COL_TILE), lambda i, j: (i, j)),
        compiler_params=_params(("arbitrary", "arbitrary")),
        name="conv",
    )(proj, proj, proj, proj, proj, proj, proj, conv_w)


def _sigmoid(x):
    return 1.0 / (1.0 + jnp.exp(-x))


def _merge_kernel(o0_ref, o1_ref, o2_ref, l0_ref, l1_ref, l2_ref, conv_ref, wao_ref, wco_ref, ga_ref, gc_ref,
                  mix_ref, attn_ref):
    @pl.when(pl.program_id(1) == 0)
    def _():
        l0, l1, l2 = l0_ref[...], l1_ref[...], l2_ref[...]
        m = jnp.maximum(jnp.maximum(l0, l1), l2)
        e0, e1, e2 = jnp.exp(l0 - m), jnp.exp(l1 - m), jnp.exp(l2 - m)
        num = e0 * o0_ref[...].astype(F32) + e1 * o1_ref[...].astype(F32) + e2 * o2_ref[...].astype(F32)
        attn_ref[...] = (num / (e0 + e1 + e2)).astype(BF16)

    ta = jnp.dot(attn_ref[...], wao_ref[...], preferred_element_type=F32)
    tc = jnp.dot(conv_ref[...], wco_ref[...], preferred_element_type=F32)
    mix = _sigmoid(ga_ref[...].astype(F32)) * ta + _sigmoid(gc_ref[...].astype(F32)) * tc
    mix_ref[...] = mix.astype(BF16)


def _merge(outs, lses, conv, w_ao, w_co, proj, seq, d):
    t = conv.shape[0]
    tm = min(1024, seq)
    dt = d // COL_TILE
    ga0 = ATTN_WIDTH * 3 // COL_TILE + 3 * dt
    gc0 = ga0 + dt
    row_blk = lambda w: pl.BlockSpec((tm, w), lambda i, j: (i, 0))
    return pl.pallas_call(
        _merge_kernel,
        out_shape=jax.ShapeDtypeStruct((t, d), BF16),
        grid=(t // tm, dt),
        in_specs=[row_blk(GROUP_WIDTH)] * 6 + [
            row_blk(d),
            pl.BlockSpec((GROUP_WIDTH, COL_TILE), lambda i, j: (0, j)),
            pl.BlockSpec((d, COL_TILE), lambda i, j: (0, j)),
            pl.BlockSpec((tm, COL_TILE), lambda i, j: (i, ga0 + j)),
            pl.BlockSpec((tm, COL_TILE), lambda i, j: (i, gc0 + j)),
        ],
        out_specs=pl.BlockSpec((tm, COL_TILE), lambda i, j: (i, j)),
        scratch_shapes=[pltpu.VMEM((tm, GROUP_WIDTH), BF16)],
        compiler_params=_params(("arbitrary", "arbitrary"), V7X_VMEM_LIMIT),
        name="merge",
    )(*outs, *lses, conv, w_ao, w_co, proj, proj)


def _oproj_kernel(mix_ref, wo_ref, x_ref, g2_ref, h_ref, xt_ref):
    h = x_ref[...] + jnp.dot(mix_ref[...], wo_ref[...], preferred_element_type=F32)
    h_ref[...] = h
    ms = jnp.mean(h * h, axis=-1, keepdims=True)
    xn = h * lax.rsqrt(ms + NORM_EPS) * g2_ref[...]
    xt_ref[...] = xn.T.astype(BF16)


def _oproj(mix, w_o, x2d, g2):
    t, d = x2d.shape
    tm = 512
    return pl.pallas_call(
        _oproj_kernel,
        out_shape=(jax.ShapeDtypeStruct((t, d), F32), jax.ShapeDtypeStruct((d, t), BF16)),
        grid=(t // tm,),
        in_specs=[
            pl.BlockSpec((tm, d), lambda i: (i, 0)),
            pl.BlockSpec((d, d), lambda i: (0, 0)),
            pl.BlockSpec((tm, d), lambda i: (i, 0)),
            pl.BlockSpec((1, d), lambda i: (0, 0)),
        ],
        out_specs=(pl.BlockSpec((tm, d), lambda i: (i, 0)), pl.BlockSpec((d, tm), lambda i: (0, i))),
        compiler_params=_params(("arbitrary",), V7X_VMEM_LIMIT),
        name="oproj",
    )(mix, w_o, x2d, g2)


def _top_values(s, n):
    vals = []
    cur = s
    for _ in range(n):
        m = jnp.max(cur, axis=0, keepdims=True)
        vals.append(m)
        cur = jnp.where(cur >= m, -jnp.inf, cur)
    return vals


def _stack_rows(rows):
    n = len(rows)
    idx = lax.broadcasted_iota(jnp.int32, (n, rows[0].shape[1]), 0)
    out = jnp.zeros((n, rows[0].shape[1]), F32)
    for i, r in enumerate(rows):
        out = jnp.where(idx == i, r, out)
    return out


def _route_kernel(xt_ref, wq_ref, keys_ref, s2_ref, p2_ref, th_ref, p1_ref):
    tm = xt_ref.shape[1]
    qt = jnp.dot(wq_ref[...], xt_ref[...], preferred_element_type=F32)
    row8 = lax.broadcasted_iota(jnp.int32, (8, tm), 0)
    for h in range(PEER_HEADS):
        q1 = qt[h * 2 * PEER_HALF:(h * 2 + 1) * PEER_HALF].astype(BF16)
        q2 = qt[(h * 2 + 1) * PEER_HALF:(h * 2 + 2) * PEER_HALF].astype(BF16)
        s1 = jnp.dot(keys_ref[2 * h], q1, preferred_element_type=F32)
        s2 = jnp.dot(keys_ref[2 * h + 1], q2, preferred_element_type=F32)
        a_rows = _top_values(s1, PEER_TOPK)
        b_rows = _top_values(s2, PEER_TOPK)
        a_arr, b_arr = _stack_rows(a_rows), _stack_rows(b_rows)
        pa_arr, pb_arr = jnp.exp(a_arr - a_rows[0]), jnp.exp(b_arr - b_rows[0])
        cand = [a_arr + b_rows[0], a_arr[:8] + b_rows[1]]
        prob = [pa_arr * pb_arr[0:1], pa_arr[:8] * pb_arr[1:2]]
        for jj in range(2, 8):
            keep = row8 < PEER_TOPK // (jj + 1)
            cand.append(jnp.where(keep, a_arr[:8] + b_rows[jj], -jnp.inf))
            prob.append(pa_arr[:8] * pb_arr[jj:jj + 1])
        cand.append(b_arr[8:] + a_rows[0])
        prob.append(pb_arr[8:])
        cand = jnp.concatenate(cand, axis=0)
        prob = jnp.concatenate(prob, axis=0)
        tau = _top_values(cand, PEER_TOPK)[PEER_TOPK - 1]
        z = jnp.sum(jnp.where(cand >= tau, prob, 0.0), axis=0, keepdims=True)
        theta = jnp.full(s1.shape, jnp.inf, F32)
        for jj in range(PEER_TOPK):
            theta = jnp.where(s1 + b_rows[jj] >= tau, b_rows[jj], theta)
        s2_ref[h] = s2
        p2_ref[h] = jnp.exp(s2 - b_rows[0])
        th_ref[h] = theta
        p1_ref[h] = jnp.exp(s1 - a_rows[0]) / z


def _route(xt, wq_t, keys):
    d, t = xt.shape
    tm = 512
    shp = jax.ShapeDtypeStruct((PEER_HEADS, PEER_NKEYS, t), F32)
    blk = pl.BlockSpec((PEER_HEADS, PEER_NKEYS, tm), lambda i: (0, 0, i))
    return pl.pallas_call(
        _route_kernel,
        out_shape=(shp, shp, shp, shp),
        grid=(t // tm,),
        in_specs=[
            pl.BlockSpec((d, tm), lambda i: (0, i)),
            pl.BlockSpec(wq_t.shape, lambda i: (0, 0)),
            pl.BlockSpec(keys.shape, lambda i: (0, 0, 0)),
        ],
        out_specs=(blk, blk, blk, blk),
        compiler_params=_params(("arbitrary",), V7X_VMEM_LIMIT),
        name="route",
    )(xt, wq_t, keys)


def _gelu_exact(x):
    return 0.5 * x * (1.0 + lax.erf(x * (0.5 ** 0.5)))


def _peer_kernel(xt_ref, u_ref, vt_ref, s2_ref, p2_ref, th_ref, p1_ref, h_ref, o_ref, acc_ref, hid_ref, a_ref):
    j = pl.program_id(1)
    n_e1 = th_ref.shape[1]

    @pl.when(j == 0)
    def _():
        acc_ref[...] = jnp.zeros_like(acc_ref)

    hid_ref[...] = jnp.dot(u_ref[...], xt_ref[...], preferred_element_type=F32)
    for e in range(n_e1):
        rows = slice(e * PEER_NKEYS, (e + 1) * PEER_NKEYS)
        w = None
        for h in range(PEER_HEADS):
            term = jnp.where(s2_ref[h] >= th_ref[h, e:e + 1, :], p2_ref[h], 0.0) * p1_ref[h, e:e + 1, :]
            w = term if w is None else w + term
        a_ref[rows, :] = (_gelu_exact(hid_ref[rows, :]) * w).astype(BF16)
    acc_ref[...] += jnp.dot(vt_ref[...], a_ref[...], preferred_element_type=F32)

    @pl.when(j == pl.num_programs(1) - 1)
    def _():
        o_ref[...] = h_ref[...] + acc_ref[...].T


def _peer(xt, u, vt, s2, p2, th, p1, h):
    d, t = xt.shape
    n_exp = u.shape[0]
    tm = 512
    ec = 1024
    n_e1 = ec // PEER_NKEYS
    full = pl.BlockSpec((PEER_HEADS, PEER_NKEYS, tm), lambda i, j: (0, 0, i))
    part = pl.BlockSpec((PEER_HEADS, n_e1, tm), lambda i, j: (0, j, i))
    return pl.pallas_call(
        _peer_kernel,
        out_shape=jax.ShapeDtypeStruct((t, d), F32),
        grid=(t // tm, n_exp // ec),
        in_specs=[
            pl.BlockSpec((d, tm), lambda i, j: (0, i)),
            pl.BlockSpec((ec, d), lambda i, j: (j, 0)),
            pl.BlockSpec((d, ec), lambda i, j: (0, j)),
            full, full, part, part,
            pl.BlockSpec((tm, d), lambda i, j: (i, 0)),
        ],
        out_specs=pl.BlockSpec((tm, d), lambda i, j: (i, 0)),
        scratch_shapes=[pltpu.VMEM((d, tm), F32), pltpu.VMEM((ec, tm), F32), pltpu.VMEM((ec, tm), BF16)],
        compiler_params=_params(("arbitrary", "arbitrary"), V7X_VMEM_LIMIT),
        name="peer",
    )(xt, u, vt, s2, p2, th, p1, h)


def _rope_tables(seq):
    half = HEAD_DIM // 2
    inv_freq = ROPE_THETA ** (-jnp.arange(half, dtype=F32) * 2.0 / HEAD_DIM)
    ang = jnp.arange(seq).astype(F32)[:, None] * inv_freq[None, :]
    cos, sin = jnp.cos(ang), jnp.sin(ang)
    return jnp.concatenate([cos, cos], axis=-1), jnp.concatenate([-sin, sin], axis=-1)


def _layer(x, w):
    batch, seq, d = x.shape
    t = batch * seq
    x2d = x.reshape(t, d)
    cos, sin = _rope_tables(seq)
    proj = _inproj(x2d, w["g1"], w["w_in"], cos, sin, w["qg"], w["kg"], seq)
    outs, lses = zip(*[_attention_group(proj, batch, seq, g) for g in range(N_GROUPS)])
    conv = _conv(proj, w["conv_w"], seq, d)
    mix = _merge(outs, lses, conv, w["w_ao"], w["w_co"], proj, seq, d)
    h, xt = _oproj(mix, w["w_o"], x2d, w["g2"])
    s2, p2, th, p1 = _route(xt, w["wq_t"], w["keys"])
    y = _peer(xt, w["u"], w["vt"], s2, p2, th, p1, h)
    return y.reshape(batch, seq, d)


def kernel(x_prompt, x_sample, norm1_g, w_in, q_norm_g, k_norm_g, w_attn_out, conv_w, w_conv_out, w_o, norm2_g,
           peer_w_q, peer_sub_keys, peer_u, peer_v):
    assert norm1_g.shape[0] == 1, "single-layer trunk"
    d = x_prompt.shape[-1]
    w = {
        "g1": norm1_g[0].reshape(1, d),
        "w_in": w_in[0].astype(BF16),
        "qg": q_norm_g[0].reshape(1, HEAD_DIM),
        "kg": k_norm_g[0].reshape(1, HEAD_DIM),
        "w_ao": w_attn_out[0].astype(BF16),
        "conv_w": conv_w[0],
        "w_co": w_conv_out[0].astype(BF16),
        "w_o": w_o[0].astype(BF16),
        "g2": norm2_g[0].reshape(1, d),
        "wq_t": peer_w_q[0].T.astype(BF16),
        "keys": peer_sub_keys[0].reshape(2 * PEER_HEADS, PEER_NKEYS, PEER_HALF).astype(BF16),
        "u": peer_u[0].astype(BF16),
        "vt": peer_v[0].T.astype(BF16),
    }
    return (_layer(x_prompt, w), _layer(x_sample, w))
```

```python
import functools

import jax
import jax.numpy as jnp
from jax import lax
from jax.experimental import pallas as pl
from jax.experimental.pallas import tpu as pltpu

F32 = jnp.float32
BF16 = jnp.bfloat16

HEAD_DIM = 128
HEADS_PER_GROUP = 4
GROUP_WIDTH = HEADS_PER_GROUP * HEAD_DIM
DILATIONS = (1, 4, 16)
N_GROUPS = len(DILATIONS)
ATTN_WIDTH = N_GROUPS * GROUP_WIDTH
N_ATTN_HEADS = ATTN_WIDTH // HEAD_DIM
BAND_HALF = 64
ROPE_THETA = 10000.0
NORM_EPS = 1e-6
PEER_HEADS = 8
PEER_NKEYS = 128
PEER_TOPK = 16
PEER_HALF = 128

COL_TILE = 512
Q_BLOCK = 128
PEER_TOKENS = 512
PEER_CHUNK = 1024
PEER_SUB = 256
V7X_VMEM_LIMIT = 56 * 1024 * 1024


def _params(sem, vmem=None):
    return pltpu.CompilerParams(dimension_semantics=sem, vmem_limit_bytes=vmem)


def _inproj_kernel(x_ref, g1_ref, w_ref, cos_ref, sin_ref, qg_ref, kg_ref, qkv_ref, rest_ref, xn_ref):
    j = pl.program_id(1)

    @pl.when(j == 0)
    def _():
        x = x_ref[...]
        ms = jnp.mean(x * x, axis=-1, keepdims=True)
        xn_ref[...] = (x * lax.rsqrt(ms + NORM_EPS) * g1_ref[...]).astype(BF16)

    acc = jnp.dot(xn_ref[...], w_ref[...], preferred_element_type=F32)
    n_qk_tiles = 2 * ATTN_WIDTH // COL_TILE
    n_qkv_tiles = 3 * ATTN_WIDTH // COL_TILE

    @pl.when(j < n_qk_tiles)
    def _():
        g = jnp.where(j < n_qk_tiles // 2, qg_ref[...], kg_ref[...])
        for hd in range(COL_TILE // HEAD_DIM):
            a = acc[:, hd * HEAD_DIM:(hd + 1) * HEAD_DIM]
            ms = jnp.mean(a * a, axis=-1, keepdims=True)
            an = a * lax.rsqrt(ms + NORM_EPS) * g
            rot = pltpu.roll(an, HEAD_DIM // 2, axis=1)
            qkv_ref[hd] = (an * cos_ref[...] + rot * sin_ref[...]).astype(BF16)

    @pl.when((j >= n_qk_tiles) & (j < n_qkv_tiles))
    def _():
        for hd in range(COL_TILE // HEAD_DIM):
            qkv_ref[hd] = acc[:, hd * HEAD_DIM:(hd + 1) * HEAD_DIM].astype(BF16)

    @pl.when(j >= n_qkv_tiles)
    def _():
        rest_ref[...] = acc.astype(BF16)


def _inproj(x2d, g1, w_in, cos, sin, qg, kg, seq):
    t, d = x2d.shape
    nc = w_in.shape[1]
    tm = min(1024, seq)
    rb = seq // tm
    n_qkv_tiles = 3 * ATTN_WIDTH // COL_TILE
    heads_per_tile = COL_TILE // HEAD_DIM
    return pl.pallas_call(
        _inproj_kernel,
        out_shape=(jax.ShapeDtypeStruct((3 * N_ATTN_HEADS, t, HEAD_DIM), BF16),
                   jax.ShapeDtypeStruct((t, nc - 3 * ATTN_WIDTH), BF16)),
        grid=(t // tm, nc // COL_TILE),
        in_specs=[
            pl.BlockSpec((tm, d), lambda i, j: (i, 0)),
            pl.BlockSpec((1, d), lambda i, j: (0, 0)),
            pl.BlockSpec((d, COL_TILE), lambda i, j: (0, j)),
            pl.BlockSpec((tm, HEAD_DIM), lambda i, j: (i % rb, 0)),
            pl.BlockSpec((tm, HEAD_DIM), lambda i, j: (i % rb, 0)),
            pl.BlockSpec((1, HEAD_DIM), lambda i, j: (0, 0)),
            pl.BlockSpec((1, HEAD_DIM), lambda i, j: (0, 0)),
        ],
        out_specs=(pl.BlockSpec((heads_per_tile, tm, HEAD_DIM), lambda i, j: (jnp.minimum(j, n_qkv_tiles - 1), i, 0)),
                   pl.BlockSpec((tm, COL_TILE), lambda i, j: (i, jnp.maximum(j - n_qkv_tiles, 0)))),
        scratch_shapes=[pltpu.VMEM((tm, d), BF16)],
        compiler_params=_params(("arbitrary", "arbitrary"), V7X_VMEM_LIMIT),
        name="inproj",
    )(x2d, g1, w_in, cos, sin, qg, kg)


def _attn_kernel(*refs, seq):
    q_refs, k_refs, v_refs, o_ref = refs[0:3], refs[3:6], refs[6:9], refs[9]
    scale = HEAD_DIM ** -0.5

    def body(qb, carry):
        q0 = pl.multiple_of(qb * Q_BLOCK, Q_BLOCK)
        ms, dens, accs = [], [], []
        for g, r in enumerate(DILATIONS):
            reach = BAND_HALF * r
            kw = min(seq, Q_BLOCK + 2 * reach)
            ks = pl.multiple_of(jnp.clip(q0 - reach, 0, seq - kw), BAND_HALF)
            q = q_refs[g][0, pl.ds(q0, Q_BLOCK), :]
            k = k_refs[g][0, pl.ds(ks, kw), :]
            v = v_refs[g][0, pl.ds(ks, kw), :]
            s = lax.dot_general(q, k, (((1,), (1,)), ((), ())), preferred_element_type=F32) * scale
            dist = (lax.broadcasted_iota(jnp.int32, (Q_BLOCK, kw), 1)
                    - lax.broadcasted_iota(jnp.int32, (Q_BLOCK, kw), 0)) + (ks - q0)
            u = jnp.abs(dist)
            valid = (u + ((u & (r - 1)) << 16)) <= reach
            s = jnp.where(valid, s, -jnp.inf)
            m = jnp.max(s, axis=-1, keepdims=True)
            p = jnp.exp(s - m)
            ms.append(m)
            dens.append(jnp.sum(p, axis=-1, keepdims=True))
            accs.append(jnp.dot(p.astype(BF16), v, preferred_element_type=F32))
        m_all = jnp.maximum(jnp.maximum(ms[0], ms[1]), ms[2])
        ws = [jnp.exp(m - m_all) for m in ms]
        num = ws[0] * accs[0] + ws[1] * accs[1] + ws[2] * accs[2]
        den = ws[0] * dens[0] + ws[1] * dens[1] + ws[2] * dens[2]
        o_ref[pl.ds(q0, Q_BLOCK), :] = (num / den).astype(BF16)
        return carry

    lax.fori_loop(0, seq // Q_BLOCK, body, 0)


def _attention(qkv, batch, seq):
    t = batch * seq
    blk = lambda first: [pl.BlockSpec((1, seq, HEAD_DIM), (lambda b, hs, f=first + g * HEADS_PER_GROUP: (f + hs, b, 0)))
                         for g in range(N_GROUPS)]
    return pl.pallas_call(
        functools.partial(_attn_kernel, seq=seq),
        out_shape=jax.ShapeDtypeStruct((t, GROUP_WIDTH), BF16),
        grid=(batch, HEADS_PER_GROUP),
        in_specs=blk(0) + blk(N_ATTN_HEADS) + blk(2 * N_ATTN_HEADS),
        out_specs=pl.BlockSpec((seq, HEAD_DIM), lambda b, hs: (b, hs)),
        compiler_params=_params(("arbitrary", "arbitrary"), V7X_VMEM_LIMIT),
        name="attn",
    )(*([qkv] * 9))


def _conv_kernel(cb_ref, cc_ref, cx_ref, ccp_ref, cxp_ref, ccn_ref, cxn_ref, w_ref, o_ref, *, blocks_per_seq):
    i = pl.program_id(0)
    tm = cc_ref.shape[0]
    z = cc_ref[...].astype(F32) * cx_ref[...].astype(F32)
    has_prev = (i % blocks_per_seq != 0).astype(F32)
    has_next = (i % blocks_per_seq != blocks_per_seq - 1).astype(F32)
    z_before = ccp_ref[7:8, :].astype(F32) * cxp_ref[7:8, :].astype(F32) * has_prev
    z_after = ccn_ref[0:1, :].astype(F32) * cxn_ref[0:1, :].astype(F32) * has_next
    row = lax.broadcasted_iota(jnp.int32, z.shape, 0)
    z_prev = jnp.where(row == 0, z_before, pltpu.roll(z, 1, axis=0))
    z_next = jnp.where(row == tm - 1, z_after, pltpu.roll(z, tm - 1, axis=0))
    zc = w_ref[0:1, :] * z_prev + w_ref[1:2, :] * z + w_ref[2:3, :] * z_next
    o_ref[...] = (cb_ref[...].astype(F32) * zc).astype(BF16)


def _conv(rest, conv_w, seq, d):
    t = rest.shape[0]
    tm = min(512, seq)
    dt = d // COL_TILE
    cb0, cc0, cx0 = 0, dt, 2 * dt
    rows8 = tm // 8
    last8 = t // 8 - 1
    prev_map = lambda off: (lambda i, j: (jnp.maximum(i * rows8 - 1, 0), off + j))
    next_map = lambda off: (lambda i, j: (jnp.minimum((i + 1) * rows8, last8), off + j))
    tile = lambda off: pl.BlockSpec((tm, COL_TILE), lambda i, j: (i, off + j))
    return pl.pallas_call(
        functools.partial(_conv_kernel, blocks_per_seq=seq // tm),
        out_shape=jax.ShapeDtypeStruct((t, d), BF16),
        grid=(t // tm, dt),
        in_specs=[
            tile(cb0), tile(cc0), tile(cx0),
            pl.BlockSpec((8, COL_TILE), prev_map(cc0)), pl.BlockSpec((8, COL_TILE), prev_map(cx0)),
            pl.BlockSpec((8, COL_TILE), next_map(cc0)), pl.BlockSpec((8, COL_TILE), next_map(cx0)),
            pl.BlockSpec((3, COL_TILE), lambda i, j: (0, j)),
        ],
        out_specs=pl.BlockSpec((tm, COL_TILE), lambda i, j: (i, j)),
        compiler_params=_params(("arbitrary", "arbitrary")),
        name="conv",
    )(rest, rest, rest, rest, rest, rest, rest, conv_w)


def _sigmoid(x):
    return 1.0 / (1.0 + jnp.exp(-x))


def _merge_kernel(attn_ref, conv_ref, wao_ref, wco_ref, ga_ref, gc_ref, mix_ref):
    ta = jnp.dot(attn_ref[...], wao_ref[...], preferred_element_type=F32)
    tc = jnp.dot(conv_ref[...], wco_ref[...], preferred_element_type=F32)
    mix = _sigmoid(ga_ref[...].astype(F32)) * ta + _sigmoid(gc_ref[...].astype(F32)) * tc
    mix_ref[...] = mix.astype(BF16)


def _merge(attn, conv, w_ao, w_co, rest, seq, d):
    t = conv.shape[0]
    tm = min(1024, seq)
    dt = d // COL_TILE
    ga0, gc0 = 3 * dt, 4 * dt
    return pl.pallas_call(
        _merge_kernel,
        out_shape=jax.ShapeDtypeStruct((t, d), BF16),
        grid=(t // tm, dt),
        in_specs=[
            pl.BlockSpec((tm, GROUP_WIDTH), lambda i, j: (i, 0)),
            pl.BlockSpec((tm, d), lambda i, j: (i, 0)),
            pl.BlockSpec((GROUP_WIDTH, COL_TILE), lambda i, j: (0, j)),
            pl.BlockSpec((d, COL_TILE), lambda i, j: (0, j)),
            pl.BlockSpec((tm, COL_TILE), lambda i, j: (i, ga0 + j)),
            pl.BlockSpec((tm, COL_TILE), lambda i, j: (i, gc0 + j)),
        ],
        out_specs=pl.BlockSpec((tm, COL_TILE), lambda i, j: (i, j)),
        compiler_params=_params(("arbitrary", "arbitrary"), V7X_VMEM_LIMIT),
        name="merge",
    )(attn, conv, w_ao, w_co, rest, rest)


def _oproj_kernel(mix_ref, wo_ref, x_ref, g2_ref, h_ref, xt_ref):
    h = x_ref[...] + jnp.dot(mix_ref[...], wo_ref[...], preferred_element_type=F32)
    h_ref[...] = h
    ms = jnp.mean(h * h, axis=-1, keepdims=True)
    xn = h * lax.rsqrt(ms + NORM_EPS) * g2_ref[...]
    xt_ref[...] = xn.T.astype(BF16)


def _oproj(mix, w_o, x2d, g2):
    t, d = x2d.shape
    tm = 512
    return pl.pallas_call(
        _oproj_kernel,
        out_shape=(jax.ShapeDtypeStruct((t, d), F32), jax.ShapeDtypeStruct((d, t), BF16)),
        grid=(t // tm,),
        in_specs=[
            pl.BlockSpec((tm, d), lambda i: (i, 0)),
            pl.BlockSpec((d, d), lambda i: (0, 0)),
            pl.BlockSpec((tm, d), lambda i: (i, 0)),
            pl.BlockSpec((1, d), lambda i: (0, 0)),
        ],
        out_specs=(pl.BlockSpec((tm, d), lambda i: (i, 0)), pl.BlockSpec((d, tm), lambda i: (0, i))),
        compiler_params=_params(("arbitrary",), V7X_VMEM_LIMIT),
        name="oproj",
    )(mix, w_o, x2d, g2)


def _top_values(s, n):
    vals = []
    cur = s
    for _ in range(n):
        m = jnp.max(cur, axis=0, keepdims=True)
        vals.append(m)
        cur = jnp.where(cur >= m, -jnp.inf, cur)
    return vals


def _stack_rows(rows):
    n = len(rows)
    idx = lax.broadcasted_iota(jnp.int32, (n, rows[0].shape[1]), 0)
    out = jnp.zeros((n, rows[0].shape[1]), F32)
    for i, r in enumerate(rows):
        out = jnp.where(idx == i, r, out)
    return out


def _route_kernel(xt_ref, wq_ref, keys_ref, s2_ref, p2_ref, th_ref, p1_ref):
    tm = xt_ref.shape[1]
    qt = jnp.dot(wq_ref[...], xt_ref[...], preferred_element_type=F32)
    row8 = lax.broadcasted_iota(jnp.int32, (8, tm), 0)
    for h in range(PEER_HEADS):
        q1 = qt[h * 2 * PEER_HALF:(h * 2 + 1) * PEER_HALF].astype(BF16)
        q2 = qt[(h * 2 + 1) * PEER_HALF:(h * 2 + 2) * PEER_HALF].astype(BF16)
        s1 = jnp.dot(keys_ref[2 * h], q1, preferred_element_type=F32)
        s2 = jnp.dot(keys_ref[2 * h + 1], q2, preferred_element_type=F32)
        a_rows = _top_values(s1, PEER_TOPK)
        b_rows = _top_values(s2, PEER_TOPK)
        a_arr, b_arr = _stack_rows(a_rows), _stack_rows(b_rows)
        pa_arr, pb_arr = jnp.exp(a_arr - a_rows[0]), jnp.exp(b_arr - b_rows[0])
        cand = [a_arr + b_rows[0], a_arr[:8] + b_rows[1]]
        prob = [pa_arr * pb_arr[0:1], pa_arr[:8] * pb_arr[1:2]]
        for jj in range(2, 8):
            keep = row8 < PEER_TOPK // (jj + 1)
            cand.append(jnp.where(keep, a_arr[:8] + b_rows[jj], -jnp.inf))
            prob.append(pa_arr[:8] * pb_arr[jj:jj + 1])
        cand.append(b_arr[8:] + a_rows[0])
        prob.append(pb_arr[8:])
        cand = jnp.concatenate(cand, axis=0)
        prob = jnp.concatenate(prob, axis=0)
        tau = _top_values(cand, PEER_TOPK)[PEER_TOPK - 1]
        z = jnp.sum(jnp.where(cand >= tau, prob, 0.0), axis=0, keepdims=True)
        theta = jnp.full(s1.shape, jnp.inf, F32)
        for jj in range(PEER_TOPK):
            theta = jnp.where(s1 + b_rows[jj] >= tau, b_rows[jj], theta)
        s2_ref[h] = s2
        p2_ref[h] = jnp.exp(s2 - b_rows[0])
        th_ref[h] = theta
        p1_ref[h] = jnp.exp(s1 - a_rows[0]) / z


def _route(xt, wq_t, keys):
    d, t = xt.shape
    tm = PEER_TOKENS
    shp = jax.ShapeDtypeStruct((PEER_HEADS, PEER_NKEYS, t), F32)
    blk = pl.BlockSpec((PEER_HEADS, PEER_NKEYS, tm), lambda i: (0, 0, i))
    return pl.pallas_call(
        _route_kernel,
        out_shape=(shp, shp, shp, shp),
        grid=(t // tm,),
        in_specs=[
            pl.BlockSpec((d, tm), lambda i: (0, i)),
            pl.BlockSpec(wq_t.shape, lambda i: (0, 0)),
            pl.BlockSpec(keys.shape, lambda i: (0, 0, 0)),
        ],
        out_specs=(blk, blk, blk, blk),
        compiler_params=_params(("arbitrary",), V7X_VMEM_LIMIT),
        name="route",
    )(xt, wq_t, keys)


def _gelu_exact(x):
    return 0.5 * x * (1.0 + lax.erf(x * (0.5 ** 0.5)))


def _peer_kernel(xt_ref, u_ref, vt_ref, s2_ref, p2_ref, th_ref, p1_ref, h_ref, o_ref, acc_ref, hid_ref, a_ref,
                 *, n_chunks):
    s = pl.program_id(0)
    slot = s % 2
    prev = jnp.maximum(s - 1, 0)
    tm = xt_ref.shape[1]
    n_e1 = th_ref.shape[1]
    rows_per_tile = 32
    lane_tiles = tm // 128

    @pl.when(s == 0)
    def _():
        a_ref[1] = jnp.zeros(a_ref.shape[1:], BF16)
        acc_ref[...] = jnp.zeros_like(acc_ref)

    xt = xt_ref[...]
    for c in range(PEER_CHUNK // PEER_SUB):
        rows = slice(c * PEER_SUB, (c + 1) * PEER_SUB)
        hid_ref[rows, :] = jnp.dot(u_ref[rows, :], xt, preferred_element_type=F32)

    carried = jnp.where(prev % n_chunks != 0, acc_ref[...], 0.0)
    acc_ref[...] = carried + jnp.dot(vt_ref[...], a_ref[1 - slot], preferred_element_type=F32)

    for tl in range(lane_tiles):
        lanes = slice(tl * 128, (tl + 1) * 128)
        for rt in range(PEER_NKEYS // rows_per_tile):
            krows = slice(rt * rows_per_tile, (rt + 1) * rows_per_tile)
            w = [None] * n_e1
            for h in range(PEER_HEADS):
                s2v = s2_ref[h, krows, lanes]
                p2v = p2_ref[h, krows, lanes]
                for e in range(n_e1):
                    term = jnp.where(s2v >= th_ref[h, e:e + 1, lanes], p2v, 0.0) * p1_ref[h, e:e + 1, lanes]
                    w[e] = term if w[e] is None else w[e] + term
            for e in range(n_e1):
                erows = slice(e * PEER_NKEYS + rt * rows_per_tile, e * PEER_NKEYS + (rt + 1) * rows_per_tile)
                a_ref[slot, erows, lanes] = (_gelu_exact(hid_ref[erows, lanes]) * w[e]).astype(BF16)

    @pl.when((s > 0) & (prev % n_chunks == n_chunks - 1))
    def _():
        o_ref[...] = h_ref[...] + acc_ref[...].T


def _peer(xt, u, vt, s2, p2, th, p1, h):
    d, t = xt.shape
    n_exp = u.shape[0]
    tm = PEER_TOKENS
    n_chunks = n_exp // PEER_CHUNK
    n_e1 = PEER_CHUNK // PEER_NKEYS
    n_steps = (t // tm) * n_chunks
    cur = lambda s: jnp.minimum(s, n_steps - 1)
    prv = lambda s: jnp.maximum(s - 1, 0)
    once = pl.Buffered(1)
    full = pl.BlockSpec((PEER_HEADS, PEER_NKEYS, tm), lambda s: (0, 0, cur(s) // n_chunks), pipeline_mode=once)
    part = pl.BlockSpec((PEER_HEADS, n_e1, tm), lambda s: (0, cur(s) % n_chunks, cur(s) // n_chunks))
    return pl.pallas_call(
        functools.partial(_peer_kernel, n_chunks=n_chunks),
        out_shape=jax.ShapeDtypeStruct((t, d), F32),
        grid=(n_steps + 1,),
        in_specs=[
            pl.BlockSpec((d, tm), lambda s: (0, cur(s) // n_chunks)),
            pl.BlockSpec((PEER_CHUNK, d), lambda s: (cur(s) % n_chunks, 0)),
            pl.BlockSpec((d, PEER_CHUNK), lambda s: (0, prv(s) % n_chunks)),
            full, full, part, part,
            pl.BlockSpec((tm, d), lambda s: (prv(s) // n_chunks, 0), pipeline_mode=once),
        ],
        out_specs=pl.BlockSpec((tm, d), lambda s: (prv(s) // n_chunks, 0)),
        scratch_shapes=[pltpu.VMEM((d, tm), F32), pltpu.VMEM((PEER_CHUNK, tm), F32),
                        pltpu.VMEM((2, PEER_CHUNK, tm), BF16)],
        compiler_params=_params(("arbitrary",), V7X_VMEM_LIMIT),
        name="peer",
    )(xt, u, vt, s2, p2, th, p1, h)


def _rope_tables(seq):
    half = HEAD_DIM // 2
    inv_freq = ROPE_THETA ** (-jnp.arange(half, dtype=F32) * 2.0 / HEAD_DIM)
    ang = jnp.arange(seq).astype(F32)[:, None] * inv_freq[None, :]
    cos, sin = jnp.cos(ang), jnp.sin(ang)
    return jnp.concatenate([cos, cos], axis=-1), jnp.concatenate([-sin, sin], axis=-1)


def _layer(x, w):
    batch, seq, d = x.shape
    t = batch * seq
    x2d = x.reshape(t, d)
    cos, sin = _rope_tables(seq)
    qkv, rest = _inproj(x2d, w["g1"], w["w_in"], cos, sin, w["qg"], w["kg"], seq)
    attn = _attention(qkv, batch, seq)
    conv = _conv(rest, w["conv_w"], seq, d)
    mix = _merge(attn, conv, w["w_ao"], w["w_co"], rest, seq, d)
    h, xt = _oproj(mix, w["w_o"], x2d, w["g2"])
    s2, p2, th, p1 = _route(xt, w["wq_t"], w["keys"])
    y = _peer(xt, w["u"], w["vt"], s2, p2, th, p1, h)
    return y.reshape(batch, seq, d)


def kernel(x_prompt, x_sample, norm1_g, w_in, q_norm_g, k_norm_g, w_attn_out, conv_w, w_conv_out, w_o, norm2_g,
           peer_w_q, peer_sub_keys, peer_u, peer_v):
    assert norm1_g.shape[0] == 1, "single-layer trunk"
    d = x_prompt.shape[-1]
    w = {
        "g1": norm1_g[0].reshape(1, d),
        "w_in": w_in[0].astype(BF16),
        "qg": q_norm_g[0].reshape(1, HEAD_DIM),
        "kg": k_norm_g[0].reshape(1, HEAD_DIM),
        "w_ao": w_attn_out[0].astype(BF16),
        "conv_w": conv_w[0],
        "w_co": w_conv_out[0].astype(BF16),
        "w_o": w_o[0].astype(BF16),
        "g2": norm2_g[0].reshape(1, d),
        "wq_t": peer_w_q[0].T.astype(BF16),
        "keys": peer_sub_keys[0].reshape(2 * PEER_HEADS, PEER_NKEYS, PEER_HALF).astype(BF16),
        "u": peer_u[0].astype(BF16),
        "vt": peer_v[0].T.astype(BF16),
    }
    return (_layer(x_prompt, w), _layer(x_sample, w))
```
